```python
import math
import jax, jax.numpy as jnp
from jax import lax
import numpy as np

D_MODEL = 4096
BATCH = 16
SEQ = 256
DEPTH = 4
DEC_BATCH = 4
DEC_SEQ = 4096
PAST_LEN = 256

GRID_W = 64
N_MIXERS = 2
N_GLA_LAYERS = (DEPTH + 1) // 2
N_FNET_LAYERS = DEPTH // 2
GLA_HEADS = 4
GLA_DK_TOT = D_MODEL // 2
GLA_DV_TOT = D_MODEL
GLA_DK = GLA_DK_TOT // GLA_HEADS
GLA_DV = GLA_DV_TOT // GLA_HEADS
GLA_GATE_RANK = 16
GLA_TAU = 16.0
GLA_CHUNK = 64
GLA_IN_COLS = 2 * GLA_DK_TOT + GLA_DV_TOT + D_MODEL
FNET_GROUPS = 8
FNET_GROUP_W = D_MODEL // FNET_GROUPS
N_EXPERTS = 64
TOP_K = 8
N_EXPERT_GROUPS = 8
TOPK_GROUPS = 4
EXPERT_FF = D_MODEL // 8
SHARED_FF = EXPERT_FF
ROUTED_SCALE = 2.5
MOE_BLOCK = 128
DEEPNORM_ALPHA = (2 * DEPTH) ** 0.25
DEEPNORM_BETA = (8 * DEPTH) ** -0.25
LN_EPS = 1e-5
N_MOD = 6

kernel_name = "hybrid_gla_fnet_moe_diffusion_step"


def layer_norm(x, g, b):
    xf = x.astype(jnp.float32)
    mu = jnp.mean(xf, axis=-1, keepdims=True)
    var = jnp.mean(jnp.square(xf - mu), axis=-1, keepdims=True)
    y = (xf - mu) * lax.rsqrt(var + LN_EPS) * g.astype(jnp.float32) + b.astype(jnp.float32)
    return y.astype(x.dtype)


def grid_pos_embed(rows, dtype):
    row = jnp.repeat(jnp.arange(rows, dtype=jnp.float32), GRID_W)
    col = jnp.tile(jnp.arange(GRID_W, dtype=jnp.float32), rows)
    quarter = D_MODEL // 4
    freqs = jnp.exp(-math.log(10000.0) * jnp.arange(quarter, dtype=jnp.float32) / quarter)

    def axis_embed(p):
        ang = p[:, None] * freqs[None, :]
        return jnp.concatenate([jnp.sin(ang), jnp.cos(ang)], axis=-1)

    return jnp.concatenate([axis_embed(row), axis_embed(col)], axis=-1).astype(dtype)


def gla_chunk_scan(q, k, v, log_a, s0):
    b, h, l, _ = q.shape
    nc = l // GLA_CHUNK

    def to_chunks(t):
        return jnp.moveaxis(t.reshape(b, h, nc, GLA_CHUNK, t.shape[-1]), 2, 0)

    mask = jnp.tril(jnp.ones((GLA_CHUNK, GLA_CHUNK), dtype=bool))

    def step(s, inp):
        qc, kc, vc, gc = inp
        cum = jnp.cumsum(gc, axis=2)
        total = cum[:, :, -1:, :]
        q_dec = qc * jnp.exp(cum)
        k_inv = kc * jnp.exp(-cum)
        scores = jnp.where(mask, jnp.einsum('bhid,bhjd->bhij', q_dec, k_inv), 0.0)
        o = jnp.einsum('bhij,bhje->bhie', scores, vc) + jnp.einsum('bhid,bhde->bhie', q_dec, s)
        k_end = kc * jnp.exp(total - cum)
        s_new = s * jnp.exp(total[:, :, 0, :])[..., None] + jnp.einsum('bhjd,bhje->bhde', k_end, vc)
        return s_new, o

    s_final, o = lax.scan(step, s0, (to_chunks(q), to_chunks(k), to_chunks(v), to_chunks(log_a)))
    o = jnp.moveaxis(o, 0, 2).reshape(b, h, l, v.shape[-1])
    return o, s_final


def gla_mixer(u, s0_fwd, s0_bwd, w_in, w_a1, w_a2, b_a, g_norm, w_out):
    b, l, _ = u.shape
    f32 = jnp.float32
    proj = u @ w_in
    q, k, v, r = jnp.split(proj, [GLA_DK_TOT, 2 * GLA_DK_TOT, 2 * GLA_DK_TOT + GLA_DV_TOT], axis=-1)

    def heads(t, dh):
        return t.reshape(b, l, GLA_HEADS, dh).transpose(0, 2, 1, 3).astype(f32)

    q = heads(q, GLA_DK) * (GLA_DK ** -0.5)
    k = heads(k, GLA_DK)
    v = heads(v, GLA_DV)
    z = jnp.einsum('bld,sdr->sblr', u, w_a1)
    g = jnp.einsum('sblr,srk->sblk', z, w_a2) + b_a[:, None, None, :]
    log_a = jax.nn.log_sigmoid(g.astype(f32)) / GLA_TAU
    log_a = log_a.reshape(2, b, l, GLA_HEADS, GLA_DK).transpose(0, 1, 3, 2, 4)

    def flip(t):
        return jnp.flip(t, axis=2)

    o_f, s_f = gla_chunk_scan(q, k, v, log_a[0], s0_fwd.astype(f32))
    o_b, s_b = gla_chunk_scan(flip(q), flip(k), flip(v), flip(log_a[1]), s0_bwd.astype(f32))
    o = o_f + flip(o_b)
    mu = jnp.mean(o, axis=-1, keepdims=True)
    var = jnp.mean(jnp.square(o - mu), axis=-1, keepdims=True)
    o = (o - mu) * lax.rsqrt(var + LN_EPS)
    o = o.transpose(0, 2, 1, 3).reshape(b, l, GLA_DV_TOT) * g_norm.astype(f32)
    o = o.astype(u.dtype) * jax.nn.silu(r)
    return o @ w_out, s_f.astype(u.dtype), s_b.astype(u.dtype)


def fnet_mixer(u, w_out):
    b, l, _ = u.shape
    ug = u.astype(jnp.float32).reshape(b, l, FNET_GROUPS, FNET_GROUP_W)
    mixed = jnp.real(jnp.fft.fft2(ug, axes=(1, 3), norm="ortho"))
    return mixed.reshape(b, l, D_MODEL).astype(u.dtype) @ w_out


def moe_ffn(u, w_router, b_router, w_eg, w_eu, w_ed, w_sg, w_su, w_sd):
    b, l, d = u.shape
    f32 = jnp.float32
    x = u.reshape(-1, d)
    n_tok = x.shape[0]
    scores = jax.nn.sigmoid((x @ w_router).astype(f32))
    biased = scores + b_router.astype(f32)
    group_score = lax.top_k(biased.reshape(n_tok, N_EXPERT_GROUPS, -1), 2)[0].sum(-1)
    _, top_groups = lax.top_k(group_score, TOPK_GROUPS)
    group_mask = jnp.any(top_groups[..., None] == jnp.arange(N_EXPERT_GROUPS)[None, None, :], axis=1)
    expert_mask = jnp.repeat(group_mask, N_EXPERTS // N_EXPERT_GROUPS, axis=1)
    _, sel = lax.top_k(jnp.where(expert_mask, biased, -jnp.inf), TOP_K)
    w_sel = jnp.take_along_axis(scores, sel, axis=1)
    gates = w_sel / jnp.sum(w_sel, axis=-1, keepdims=True) * ROUTED_SCALE

    n_assign = n_tok * TOP_K
    e_flat = sel.reshape(-1).astype(jnp.int32)
    tok_flat = jnp.repeat(jnp.arange(n_tok, dtype=jnp.int32), TOP_K)
    g_flat = gates.reshape(-1)
    order = jnp.argsort(e_flat)
    e_sorted = e_flat[order]
    counts = jnp.zeros((N_EXPERTS,), jnp.int32).at[e_flat].add(1)
    padded = (counts + MOE_BLOCK - 1) // MOE_BLOCK * MOE_BLOCK
    starts = jnp.cumsum(counts) - counts
    pad_ends = jnp.cumsum(padded)
    pad_starts = pad_ends - padded
    dest = pad_starts[e_sorted] + (jnp.arange(n_assign, dtype=jnp.int32) - starts[e_sorted])
    n_blocks = -(-(n_assign + N_EXPERTS * (MOE_BLOCK - 1)) // MOE_BLOCK)
    n_rows = n_blocks * MOE_BLOCK
    buf_tok = jnp.full((n_rows,), n_tok, jnp.int32).at[dest].set(tok_flat[order])
    buf_gate = jnp.zeros((n_rows,), f32).at[dest].set(g_flat[order])
    block_start = jnp.arange(n_blocks, dtype=jnp.int32) * MOE_BLOCK
    block_expert = jnp.minimum(jnp.searchsorted(pad_ends, block_start, side='right'), N_EXPERTS - 1)
    x_pad = jnp.concatenate([x, jnp.zeros((1, d), x.dtype)], axis=0)

    def expert_block(acc, args):
        toks, gts, e = args
        xb = x_pad[toks]
        hb = jax.nn.silu(xb @ w_eg[e]) * (xb @ w_eu[e])
        yb = (hb @ w_ed[e]).astype(f32) * gts[:, None]
        return acc.at[toks].add(yb), None

    routed, _ = lax.scan(expert_block, jnp.zeros((n_tok + 1, d), f32),
                         (buf_tok.reshape(n_blocks, MOE_BLOCK), buf_gate.reshape(n_blocks, MOE_BLOCK), block_expert))
    shared = (jax.nn.silu(x @ w_sg) * (x @ w_su)) @ w_sd
    return (routed[:n_tok].astype(x.dtype) + shared).reshape(b, l, d)


def trunk(h, cond, s0_all, w_ada, b_ada, w_gla_in, w_gla_a1, w_gla_a2, b_gla_a, gla_norm_g, w_gla_out,
          w_fnet_out, ln_g, ln_b, w_router, b_router, w_exp_gate, w_exp_up, w_exp_down,
          w_sh_gate, w_sh_up, w_sh_down):
    silu_c = jax.nn.silu(cond)
    final_states = []
    for i in range(DEPTH):
        mod = silu_c @ w_ada[i] + b_ada[i]
        shift1, scale1, gate1, shift2, scale2, gate2 = jnp.split(mod[:, None, :], N_MOD, axis=-1)
        u = h * (1 + scale1) + shift1
        j = i // N_MIXERS
        if i % N_MIXERS == 0:
            out, s_f, s_b = gla_mixer(u, s0_all[:, j, 0], s0_all[:, j, 1], w_gla_in[j], w_gla_a1[j],
                                      w_gla_a2[j], b_gla_a[j], gla_norm_g[j], w_gla_out[j])
            final_states.append(jnp.stack([s_f, s_b], axis=1))
        else:
            out = fnet_mixer(u, w_fnet_out[j])
        h = layer_norm(DEEPNORM_ALPHA * h + gate1 * out, ln_g[i, 0], ln_b[i, 0])
        u = h * (1 + scale2) + shift2
        ffn = moe_ffn(u, w_router[i], b_router[i], w_exp_gate[i], w_exp_up[i], w_exp_down[i],
                      w_sh_gate[i], w_sh_up[i], w_sh_down[i])
        h = layer_norm(DEEPNORM_ALPHA * h + gate2 * ffn, ln_g[i, 1], ln_b[i, 1])
    return h, jnp.stack(final_states, axis=1)


def setup_inputs(seed: int = 0) -> dict:
    key = jax.random.key(seed)
    ks = jax.random.split(key, 32)
    f32 = jnp.float32

    def nrm(idx, shape, scale):
        return jax.random.normal(ks[idx], shape, f32) * scale

    d = D_MODEL
    w_qk = nrm(5, (N_GLA_LAYERS, d, 2 * GLA_DK_TOT), d ** -0.5)
    w_v = nrm(6, (N_GLA_LAYERS, d, GLA_DV_TOT), d ** -0.5 * DEEPNORM_BETA)
    w_r = nrm(7, (N_GLA_LAYERS, d, d), d ** -0.5)
    return {
        "x_prompt": nrm(0, (BATCH, SEQ, d), 1.0),
        "x_sample": nrm(1, (DEC_BATCH, DEC_SEQ, d), 1.0),
        "c": nrm(2, (DEC_BATCH, d), 1.0),
        "state_gla": nrm(3, (DEC_BATCH, N_GLA_LAYERS, 2, GLA_HEADS, GLA_DK, GLA_DV), 1.0),
        "c_ctx": nrm(4, (d,), 1.0),
        "ln_in_g": 1.0 + nrm(8, (d,), 0.02),
        "ln_in_b": nrm(9, (d,), 0.02),
        "w_ada": nrm(10, (DEPTH, d, N_MOD * d), 0.5 * d ** -0.5),
        "b_ada": nrm(11, (DEPTH, N_MOD * d), 0.02),
        "w_gla_in": jnp.concatenate([w_qk, w_v, w_r], axis=-1),
        "w_gla_a1": nrm(12, (N_GLA_LAYERS, 2, d, GLA_GATE_RANK), d ** -0.5),
        "w_gla_a2": nrm(13, (N_GLA_LAYERS, 2, GLA_GATE_RANK, GLA_DK_TOT), GLA_GATE_RANK ** -0.5),
        "b_gla_a": nrm(14, (N_GLA_LAYERS, 2, GLA_DK_TOT), 0.1),
        "gla_norm_g": 1.0 + nrm(15, (N_GLA_LAYERS, GLA_DV_TOT), 0.02),
        "w_gla_out": nrm(16, (N_GLA_LAYERS, GLA_DV_TOT, d), GLA_DV_TOT ** -0.5 * DEEPNORM_BETA),
        "w_fnet_out": nrm(17, (N_FNET_LAYERS, d, d), d ** -0.5 * DEEPNORM_BETA),
        "ln_g": 1.0 + nrm(18, (DEPTH, 2, d), 0.02),
        "ln_b": nrm(19, (DEPTH, 2, d), 0.02),
        "w_router": nrm(20, (DEPTH, d, N_EXPERTS), d ** -0.5),
        "b_router": nrm(21, (DEPTH, N_EXPERTS), 0.01),
        "w_exp_gate": nrm(22, (DEPTH, N_EXPERTS, d, EXPERT_FF), d ** -0.5),
        "w_exp_up": nrm(23, (DEPTH, N_EXPERTS, d, EXPERT_FF), d ** -0.5),
        "w_exp_down": nrm(24, (DEPTH, N_EXPERTS, EXPERT_FF, d), EXPERT_FF ** -0.5 * DEEPNORM_BETA),
        "w_sh_gate": nrm(25, (DEPTH, d, SHARED_FF), d ** -0.5),
        "w_sh_up": nrm(26, (DEPTH, d, SHARED_FF), d ** -0.5),
        "w_sh_down": nrm(27, (DEPTH, SHARED_FF, d), SHARED_FF ** -0.5 * DEEPNORM_BETA),
    }


def reference(x_prompt, x_sample, c, state_gla, c_ctx, ln_in_g, ln_in_b, w_ada, b_ada, w_gla_in, w_gla_a1,
              w_gla_a2, b_gla_a, gla_norm_g, w_gla_out, w_fnet_out, ln_g, ln_b, w_router, b_router,
              w_exp_gate, w_exp_up, w_exp_down, w_sh_gate, w_sh_up, w_sh_down):
    weights = (w_ada, b_ada, w_gla_in, w_gla_a1, w_gla_a2, b_gla_a, gla_norm_g, w_gla_out, w_fnet_out,
               ln_g, ln_b, w_router, b_router, w_exp_gate, w_exp_up, w_exp_down, w_sh_gate, w_sh_up, w_sh_down)
    n_req = x_prompt.shape[0]
    h_ctx = layer_norm(x_prompt, ln_in_g, ln_in_b)
    cond_ctx = jnp.broadcast_to(c_ctx[None, :], (n_req, D_MODEL))
    s0_ctx = jnp.zeros((n_req, N_GLA_LAYERS, 2, GLA_HEADS, GLA_DK, GLA_DV), x_prompt.dtype)
    y_prompt, new_state_gla = trunk(h_ctx, cond_ctx, s0_ctx, *weights)
    rows = x_sample.shape[1] // GRID_W
    h_lat = layer_norm(x_sample + grid_pos_embed(rows, x_sample.dtype)[None], ln_in_g, ln_in_b)
    y_sample, _ = trunk(h_lat, c, state_gla, *weights)
    return (y_prompt, y_sample, new_state_gla)
```

```python
import functools
import math

import jax
import jax.numpy as jnp
from jax import lax
from jax.experimental import pallas as pl
from jax.experimental.pallas import tpu as pltpu

F32 = jnp.float32
BF16 = jnp.bfloat16

TOKEN_TILE = 256
V7X_VMEM_LIMIT_BYTES = 56 * 1024 * 1024
GRID_W = 64
GLA_TAU = 16.0
GLA_CHUNK = 64
FNET_GROUPS = 8
TOP_K = 8
N_EXPERT_GROUPS = 8
TOPK_GROUPS = 4
ROUTED_SCALE = 2.5
MOE_BLOCK = 256
LN_EPS = 1e-5
N_MOD = 6


def _params(n_axes):
    return pltpu.CompilerParams(dimension_semantics=("arbitrary",) * n_axes,
                                vmem_limit_bytes=V7X_VMEM_LIMIT_BYTES)


def _dot(a, b):
    return jnp.dot(a, b, preferred_element_type=F32)


def _silu(x):
    return x / (1.0 + jnp.exp(-x))


def _layer_norm(x, g, b):
    mu = jnp.mean(x, axis=-1, keepdims=True)
    xc = x - mu
    var = jnp.mean(xc * xc, axis=-1, keepdims=True)
    return xc * lax.rsqrt(var + LN_EPS) * g + b


class _Stream:
    def __init__(self, n_ctx_seq, ctx_len, n_lat_seq, lat_len):
        assert ctx_len % TOKEN_TILE == 0 and lat_len % TOKEN_TILE == 0
        self.n_ctx_seq, self.ctx_len = n_ctx_seq, ctx_len
        self.n_lat_seq, self.lat_len = n_lat_seq, lat_len
        self.n_ctx = n_ctx_seq * ctx_len
        self.n_tok = self.n_ctx + n_lat_seq * lat_len
        self.ctx_tiles = self.n_ctx // TOKEN_TILE
        self.lat_seq_tiles = lat_len // TOKEN_TILE
        self.n_tiles = self.n_tok // TOKEN_TILE

    def mod_row(self, t):
        return jnp.where(t < self.ctx_tiles, 0, 1 + (t - self.ctx_tiles) // self.lat_seq_tiles)


def _mod_spec(stream, d, col, rows_per_step=1):
    return pl.BlockSpec((None, None, 1, d),
                        lambda t: (stream.mod_row(t * rows_per_step), col, 0, 0))


def _row_spec(d):
    return pl.BlockSpec((1, d), lambda t: (0, 0))


def _ada_kernel(c_ref, w_ref, b_ref, o_ref):
    s = _silu(c_ref[...]).astype(BF16)
    o_ref[...] = _dot(s, w_ref[...].astype(BF16)) + b_ref[...]


def _ada_mod(cond8, w_ada, b_ada):
    depth, d, n_out = w_ada.shape
    tn = min(512, n_out)
    return pl.pallas_call(
        _ada_kernel,
        grid=(depth, n_out // tn),
        in_specs=[pl.BlockSpec((8, d), lambda l, j: (0, 0)),
                  pl.BlockSpec((None, d, tn), lambda l, j: (l, 0, j)),
                  pl.BlockSpec((None, 1, tn), lambda l, j: (l, 0, j))],
        out_specs=pl.BlockSpec((None, 8, tn), lambda l, j: (l, 0, j)),
        out_shape=jax.ShapeDtypeStruct((depth, 8, n_out), F32),
        compiler_params=_params(2),
        name="ada_mod",
    )(cond8, w_ada, b_ada.reshape(depth, 1, n_out))


def _embed_kernel(x_ref, g_ref, b_ref, sc_ref, sh_ref, h_ref, u_ref, tab_ref, *, stream):
    t = pl.program_id(0)
    d = x_ref.shape[1]
    quarter = d // 4
    half = 2 * quarter

    @pl.when(t == 0)
    def _():
        j = lax.broadcasted_iota(jnp.int32, (GRID_W, quarter), 1).astype(F32)
        p = lax.broadcasted_iota(jnp.int32, (GRID_W, quarter), 0).astype(F32)
        ang = p * jnp.exp((-math.log(10000.0) * j) / quarter)
        tab_ref[:, :quarter] = jnp.sin(ang)
        tab_ref[:, quarter:] = jnp.cos(ang)

    is_lat = (t >= stream.ctx_tiles).astype(F32)
    seq_tile = jnp.maximum(t - stream.ctx_tiles, 0) % stream.lat_seq_tiles
    rows_per_tile = TOKEN_TILE // GRID_W
    col_emb = tab_ref[...]
    for j in range(rows_per_tile):
        row_emb = tab_ref[pl.ds(seq_tile * rows_per_tile + j, 1), :]
        pos = jnp.concatenate([jnp.broadcast_to(row_emb, (GRID_W, half)), col_emb], axis=1)
        rows = slice(j * GRID_W, (j + 1) * GRID_W)
        y = _layer_norm(x_ref[rows, :] + is_lat * pos, g_ref[...], b_ref[...])
        h_ref[rows, :] = y
        u_ref[rows, :] = (y * (1.0 + sc_ref[...]) + sh_ref[...]).astype(BF16)


def _embed(stream, x_all, ln_g, ln_b, mod_l):
    n, d = x_all.shape
    assert stream.lat_len // GRID_W <= GRID_W
    tile = pl.BlockSpec((TOKEN_TILE, d), lambda t: (t, 0))
    return pl.pallas_call(
        functools.partial(_embed_kernel, stream=stream),
        grid=(stream.n_tiles,),
        in_specs=[tile, _row_spec(d), _row_spec(d), _mod_spec(stream, d, 1), _mod_spec(stream, d, 0)],
        out_specs=[tile, tile],
        out_shape=[jax.ShapeDtypeStruct((n, d), F32), jax.ShapeDtypeStruct((n, d), BF16)],
        scratch_shapes=[pltpu.VMEM((GRID_W, d // 2), F32)],
        compiler_params=_params(1),
        name="embed_ln",
    )(x_all, ln_g.reshape(1, d), ln_b.reshape(1, d), mod_l, mod_l)


def _lnres_kernel(*refs, n_add, has_next, alpha):
    h_ref = refs[0]
    add_refs = refs[1:1 + n_add]
    gate_ref, g_ref, b_ref = refs[1 + n_add:4 + n_add]
    rest = refs[4 + n_add:]
    a = add_refs[0][...].astype(F32)
    for r in add_refs[1:]:
        a = a + r[...].astype(F32)
    y = _layer_norm(alpha * h_ref[...] + gate_ref[...] * a, g_ref[...], b_ref[...])
    if has_next:
        sc_ref, sh_ref, hn_ref, u_ref = rest
        u_ref[...] = (y * (1.0 + sc_ref[...]) + sh_ref[...]).astype(BF16)
    else:
        (hn_ref,) = rest
    hn_ref[...] = y


def _lnres(stream, h, adds, mod_l, gate_col, ln_g, ln_b, alpha, next_mod=None, next_cols=None):
    n, d = h.shape
    tile = pl.BlockSpec((TOKEN_TILE, d), lambda t: (t, 0))
    has_next = next_mod is not None
    in_specs = [tile] + [tile] * len(adds) + [_mod_spec(stream, d, gate_col), _row_spec(d), _row_spec(d)]
    args = [h, *adds, mod_l, ln_g.reshape(1, d), ln_b.reshape(1, d)]
    out_specs = [tile]
    out_shape = [jax.ShapeDtypeStruct((n, d), F32)]
    if has_next:
        in_specs += [_mod_spec(stream, d, next_cols[1]), _mod_spec(stream, d, next_cols[0])]
        args += [next_mod, next_mod]
        out_specs.append(tile)
        out_shape.append(jax.ShapeDtypeStruct((n, d), BF16))
    out = pl.pallas_call(
        functools.partial(_lnres_kernel, n_add=len(adds), has_next=has_next, alpha=alpha),
        grid=(stream.n_tiles,),
        in_specs=in_specs, out_specs=out_specs, out_shape=out_shape,
        compiler_params=_params(1),
        name="residual_ln",
    )(*args)
    return (out[0], out[1]) if has_next else (out[0], None)


def _mm_kernel(x_ref, w_ref, o_ref, wbf_ref):
    @pl.when(pl.program_id(1) == 0)
    def _():
        wbf_ref[...] = w_ref[...].astype(BF16)

    o_ref[...] = _dot(x_ref[...], wbf_ref[...]).astype(o_ref.dtype)


def _matmul(x, w, out_dtype, tm=512, tn=512):
    n, k = x.shape
    n_out = w.shape[1]
    tm, tn = min(tm, n), min(tn, n_out)
    assert n % tm == 0 and n_out % tn == 0
    return pl.pallas_call(
        _mm_kernel,
        grid=(n_out // tn, n // tm),
        in_specs=[pl.BlockSpec((tm, k), lambda j, i: (i, 0)),
                  pl.BlockSpec((k, tn), lambda j, i: (0, j))],
        out_specs=pl.BlockSpec((tm, tn), lambda j, i: (i, j)),
        out_shape=jax.ShapeDtypeStruct((n, n_out), out_dtype),
        scratch_shapes=[pltpu.VMEM((k, tn), BF16)],
        compiler_params=_params(2),
        name="matmul",
    )(x, w)


def _gla_kernel(q_ref, k_ref, v_ref, r_ref, z_ref, w2_ref, ba_ref, gn_ref, *rest,
                nb, has_s0, write_state, scale):
    rest = list(rest)
    s0_ref = rest.pop(0) if has_s0 else None
    o_ref = rest.pop(0)
    sf_ref = rest.pop(0) if write_state else None
    state_ref, ofwd_ref = rest
    t = pl.program_id(2)
    tm = q_ref.shape[0]
    c = GLA_CHUNK

    @pl.when((t == 0) | (t == nb))
    def _():
        state_ref[...] = s0_ref[...] if has_s0 else jnp.zeros_like(state_ref)

    ii = lax.broadcasted_iota(jnp.int32, (c, c), 0)
    jj = lax.broadcasted_iota(jnp.int32, (c, c), 1)

    def run(bwd):
        blk = (2 * nb - 1 - t) if bwd else t
        mask = (jj >= ii) if bwd else (jj <= ii)
        tri = mask.astype(BF16)
        chunks = range(tm // c)
        for ci in (reversed(chunks) if bwd else chunks):
            rows = slice(ci * c, (ci + 1) * c)
            g = _dot(z_ref[rows, :].astype(BF16), w2_ref[...]) + ba_ref[...]
            log_a = (jnp.minimum(g, 0.0) - jnp.log(1.0 + jnp.exp(-jnp.abs(g)))) * (1.0 / GLA_TAU)
            hi = log_a.astype(BF16)
            r1 = log_a - hi.astype(F32)
            mid = r1.astype(BF16)
            lo = (r1 - mid.astype(F32)).astype(BF16)
            cum = _dot(tri, hi) + _dot(tri, mid) + _dot(tri, lo)
            total = cum[0:1, :] if bwd else cum[c - 1:c, :]
            q = q_ref[rows, :].astype(F32)
            k = k_ref[rows, :].astype(F32)
            v = v_ref[rows, :]
            q_dec = (q * jnp.exp(cum) * scale).astype(BF16)
            k_inv = (k * jnp.exp(-cum)).astype(BF16)
            k_end = (k * jnp.exp(total - cum)).astype(BF16)
            scores = lax.dot_general(q_dec, k_inv, (((1,), (1,)), ((), ())), preferred_element_type=F32)
            scores = jnp.where(mask, scores, 0.0).astype(BF16)
            state = state_ref[...]
            o = _dot(scores, v) + _dot(q_dec, state.astype(BF16))
            kv = lax.dot_general(k_end, v, (((0,), (0,)), ((), ())), preferred_element_type=F32)
            state_ref[...] = state * jnp.transpose(jnp.exp(total)) + kv
            seq_rows = pl.ds(pl.multiple_of(blk * tm + ci * c, c), c)
            if not bwd:
                ofwd_ref[seq_rows, :] = o
            else:
                o = ofwd_ref[seq_rows, :] + o
                mu = jnp.mean(o, axis=-1, keepdims=True)
                oc = o - mu
                var = jnp.mean(oc * oc, axis=-1, keepdims=True)
                y = oc * lax.rsqrt(var + LN_EPS) * gn_ref[...]
                o_ref[rows, :] = (y * _silu(r_ref[rows, :].astype(F32))).astype(BF16)

    pl.when(t < nb)(lambda: run(False))
    pl.when(t >= nb)(lambda: run(True))

    if write_state:
        @pl.when((t == nb - 1) | (t == 2 * nb - 1))
        def _():
            sf_ref[...] = state_ref[...]


def _gla_scan(proj, z, w2ext, b_a, g_norm, s0, o_prev, *, n_seq, seq_len, tile0, heads, dk, dv,
              layer, write_state):
    n = proj.shape[0]
    tm = TOKEN_TILE
    nb = seq_len // tm
    k_col0, v_col0, r_col0 = heads, (2 * heads * dk) // dv, (2 * heads * dk + heads * dv) // dv
    assert (2 * heads * dk) % dv == 0

    def blk(t):
        return jnp.where(t < nb, t, 2 * nb - 1 - t)

    def tok(b, t):
        return tile0 + b * nb + blk(t)

    def out_tok(b, t):
        return tile0 + b * nb + jnp.where(t < nb, nb - 1, 2 * nb - 1 - t)

    in_specs = [
        pl.BlockSpec((tm, dk), lambda b, h, t: (tok(b, t), h)),
        pl.BlockSpec((tm, dk), lambda b, h, t: (tok(b, t), k_col0 + h)),
        pl.BlockSpec((tm, dv), lambda b, h, t: (tok(b, t), v_col0 + h)),
        pl.BlockSpec((tm, dv), lambda b, h, t: (tok(b, t), r_col0 + h)),
        pl.BlockSpec((tm, 128), lambda b, h, t: (tok(b, t), 0)),
        pl.BlockSpec((None, 128, dk), lambda b, h, t: (t // nb, 0, h)),
        pl.BlockSpec((None, 1, dk), lambda b, h, t: (t // nb, 0, h)),
        pl.BlockSpec((1, dv), lambda b, h, t: (0, h)),
    ]
    args = [proj, proj, proj, proj, z, w2ext, b_a.reshape(2, 1, heads * dk), g_norm.reshape(1, heads * dv)]
    if s0 is not None:
        in_specs.append(pl.BlockSpec((None, None, None, None, dk, dv),
                                     lambda b, h, t: (b, layer, t // nb, h, 0, 0)))
        args.append(s0)
    aliases = {}
    if o_prev is not None:
        in_specs.append(pl.BlockSpec(memory_space=pl.ANY))
        args.append(o_prev)
        aliases = {len(args) - 1: 0}
    out_specs = [pl.BlockSpec((tm, dv), lambda b, h, t: (out_tok(b, t), h))]
    out_shape = [jax.ShapeDtypeStruct((n, heads * dv), BF16)]
    if write_state:
        out_specs.append(pl.BlockSpec((None, None, None, dk, dv), lambda b, h, t: (b, t // nb, h, 0, 0)))
        out_shape.append(jax.ShapeDtypeStruct((n_seq, 2, heads, dk, dv), F32))

    def body(*refs):
        refs = list(refs)
        if o_prev is not None:
            refs.pop(len(args) - 1)
        _gla_kernel(*refs, nb=nb, has_s0=s0 is not None, write_state=write_state, scale=dk ** -0.5)

    out = pl.pallas_call(
        body,
        grid=(n_seq, heads, 2 * nb),
        in_specs=in_specs, out_specs=out_specs, out_shape=out_shape,
        scratch_shapes=[pltpu.VMEM((dk, dv), F32), pltpu.VMEM((seq_len, dv), F32)],
        input_output_aliases=aliases,
        compiler_params=_params(3),
        name="gla_scan",
    )(*args)
    return out


def _gla_mixer(stream, u, s0_lat, layer, w_in, w_a1, w_a2, b_a, g_norm, w_out, heads, dk, dv):
    d = u.shape[1]
    rank = w_a1.shape[-1]
    assert 2 * rank <= 128
    proj = _matmul(u, w_in, BF16)
    w_a1_cat = jnp.zeros((d, 128), F32).at[:, :rank].set(w_a1[0]).at[:, rank:2 * rank].set(w_a1[1])
    z = _matmul(u, w_a1_cat, F32)
    w2ext = jnp.zeros((2, 128, heads * dk), F32)
    w2ext = w2ext.at[0, :rank].set(w_a2[0]).at[1, rank:2 * rank].set(w_a2[1]).astype(BF16)
    common = dict(heads=heads, dk=dk, dv=dv, layer=layer)
    o, s_ctx = _gla_scan(proj, z, w2ext, b_a, g_norm, None, None, n_seq=stream.n_ctx_seq,
                         seq_len=stream.ctx_len, tile0=0, write_state=True, **common)
    (o,) = _gla_scan(proj, z, w2ext, b_a, g_norm, s0_lat, o, n_seq=stream.n_lat_seq,
                     seq_len=stream.lat_len, tile0=stream.ctx_tiles, write_state=False, **common)
    return _matmul(o, w_out, BF16), s_ctx


def _dft_tables(n, scale):
    idx = jnp.arange(n, dtype=jnp.int32)
    ang = ((idx[:, None] * idx[None, :]) % n).astype(F32) * (2.0 * math.pi / n)
    return jnp.cos(ang) * scale, jnp.sin(ang) * scale


def _fnet_chan_kernel(x_ref, t_ref, a_ref, b_ref):
    w = t_ref.shape[0]
    for g in range(FNET_GROUPS):
        cols = slice(g * w, (g + 1) * w)
        ab = _dot(x_ref[:, cols], t_ref[...])
        a_ref[:, cols] = ab[:, :w].astype(BF16)
        b_ref[:, cols] = ab[:, w:].astype(BF16)


def _fnet_chan(u, table):
    n, d = u.shape
    tm = min(512, n)
    tile = pl.BlockSpec((tm, d), lambda i: (i, 0))
    return pl.pallas_call(
        _fnet_chan_kernel,
        grid=(n // tm,),
        in_specs=[tile, pl.BlockSpec(table.shape, lambda i: (0, 0))],
        out_specs=[tile, tile],
        out_shape=[jax.ShapeDtypeStruct((n, d), BF16)] * 2,
        compiler_params=_params(1),
        name="fnet_channel_dft",
    )(u, table)


def _fnet_seq_kernel(tc_ref, ts_ref, a_ref, b_ref, *rest):
    o_ref = rest[-1]
    o_ref[...] = (_dot(tc_ref[...], a_ref[...]) + _dot(ts_ref[...], b_ref[...])).astype(BF16)


def _fnet_seq(a, b, tc, ts, o_prev, *, n_seq, seq_len, row0):
    n, d = a.shape
    tm = min(512, seq_len)
    tn = min(512, d)
    assert row0 % seq_len == 0
    seq0 = row0 // seq_len
    nrb = seq_len // tm
    in_specs = [pl.BlockSpec((tm, seq_len), lambda s, j, i: (i, 0)),
                pl.BlockSpec((tm, seq_len), lambda s, j, i: (i, 0)),
                pl.BlockSpec((seq_len, tn), lambda s, j, i: (seq0 + s, j)),
                pl.BlockSpec((seq_len, tn), lambda s, j, i: (seq0 + s, j))]
    args = [tc, ts, a, b]
    aliases = {}
    if o_prev is not None:
        in_specs.append(pl.BlockSpec(memory_space=pl.ANY))
        args.append(o_prev)
        aliases = {4: 0}
    return pl.pallas_call(
        _fnet_seq_kernel,
        grid=(n_seq, d // tn, nrb),
        in_specs=in_specs,
        out_specs=pl.BlockSpec((tm, tn), lambda s, j, i: ((seq0 + s) * nrb + i, j)),
        out_shape=jax.ShapeDtypeStruct((n, d), BF16),
        input_output_aliases=aliases,
        compiler_params=_params(3),
        name="fnet_sequence_dft",
    )(*args)


def _fnet_mixer(stream, u, w_out):
    d = u.shape[1]
    w = d // FNET_GROUPS
    cw, sw = _dft_tables(w, w ** -0.5)
    a, b = _fnet_chan(u, jnp.concatenate([cw, sw], axis=1).astype(BF16))
    mixed = None
    for n_seq, seq_len, row0 in ((stream.n_ctx_seq, stream.ctx_len, 0),
                                 (stream.n_lat_seq, stream.lat_len, stream.n_ctx)):
        cl, sl = _dft_tables(seq_len, seq_len ** -0.5)
        mixed = _fnet_seq(a, b, cl.astype(BF16), (-sl).astype(BF16), mixed,
                          n_seq=n_seq, seq_len=seq_len, row0=row0)
    return _matmul(mixed, w_out, BF16)


def _router_kernel(h_ref, sc_ref, sh_ref, whi_ref, wlo_ref, o_ref):
    u = h_ref[...] * (1.0 + sc_ref[...]) + sh_ref[...]
    u_hi = u.astype(BF16)
    u_lo = (u - u_hi.astype(F32)).astype(BF16)
    logits = _dot(u_hi, whi_ref[...]) + _dot(u_hi, wlo_ref[...]) + _dot(u_lo, whi_ref[...])
    o_ref[...] = 1.0 / (1.0 + jnp.exp(-logits))


def _router_scores(stream, h, mod_l, w_router):
    n, d = h.shape
    e = w_router.shape[1]
    assert e <= 128
    w_pad = jnp.zeros((d, 128), F32).at[:, :e].set(w_router)
    w_hi = w_pad.astype(BF16)
    w_lo = (w_pad - w_hi.astype(F32)).astype(BF16)
    wspec = pl.BlockSpec((d, 128), lambda t: (0, 0))
    out = pl.pallas_call(
        _router_kernel,
        grid=(stream.n_tiles,),
        in_specs=[pl.BlockSpec((TOKEN_TILE, d), lambda t: (t, 0)),
                  _mod_spec(stream, d, 4), _mod_spec(stream, d, 3), wspec, wspec],
        out_specs=pl.BlockSpec((TOKEN_TILE, 128), lambda t: (t, 0)),
        out_shape=jax.ShapeDtypeStruct((n, 128), F32),
        compiler_params=_params(1),
        name="moe_router",
    )(h, mod_l, mod_l, w_hi, w_lo)
    return out[:, :e]


def _swiglu(x, wg, wu, wd):
    g = _dot(x, wg)
    hidden = (_silu(g) * _dot(x, wu)).astype(BF16)
    return _dot(hidden, wd)


def _expert_kernel(be_ref, act_ref, x_ref, wg_ref, wu_ref, wd_ref, o_ref):
    b = pl.program_id(0)

    @pl.when(act_ref[b] != 0)
    def _():
        o_ref[...] = _swiglu(x_ref[...], wg_ref[...], wu_ref[...], wd_ref[...]).astype(BF16)

    @pl.when(act_ref[b] == 0)
    def _():
        o_ref[...] = jnp.zeros_like(o_ref)


def _expert_ffn(x_sorted, block_expert, block_active, wg, wu, wd):
    n_rows, d = x_sorted.shape
    e, _, f = wg.shape
    tile = pl.BlockSpec((MOE_BLOCK, d), lambda b, be, act: (b, 0))
    grid_spec = pltpu.PrefetchScalarGridSpec(
        num_scalar_prefetch=2,
        grid=(n_rows // MOE_BLOCK,),
        in_specs=[tile,
                  pl.BlockSpec((None, d, f), lambda b, be, act: (be[b], 0, 0)),
                  pl.BlockSpec((None, d, f), lambda b, be, act: (be[b], 0, 0)),
                  pl.BlockSpec((None, f, d), lambda b, be, act: (be[b], 0, 0))],
        out_specs=tile,
    )
    return pl.pallas_call(
        _expert_kernel,
        grid_spec=grid_spec,
        out_shape=jax.ShapeDtypeStruct((n_rows, d), BF16),
        compiler_params=_params(1),
        name="moe_experts",
    )(block_expert, block_active, x_sorted, wg, wu, wd)


def _shared_kernel(x_ref, wg_ref, wu_ref, wd_ref, o_ref):
    o_ref[...] = _swiglu(x_ref[...], wg_ref[...], wu_ref[...], wd_ref[...]).astype(BF16)


def _shared_ffn(u, wg, wu, wd):
    n, d = u.shape
    f = wg.shape[1]
    tm = min(512, n)
    tile = pl.BlockSpec((tm, d), lambda i: (i, 0))
    return pl.pallas_call(
        _shared_kernel,
        grid=(n // tm,),
        in_specs=[tile, pl.BlockSpec((d, f), lambda i: (0, 0)), pl.BlockSpec((d, f), lambda i: (0, 0)),
                  pl.BlockSpec((f, d), lambda i: (0, 0))],
        out_specs=tile,
        out_shape=jax.ShapeDtypeStruct((n, d), BF16),
        compiler_params=_params(1),
        name="moe_shared_expert",
    )(u, wg.astype(BF16), wu.astype(BF16), wd.astype(BF16))


def _route(scores, b_router):
    n_tok, n_exp = scores.shape
    biased = scores + b_router.astype(F32)
    group_score = lax.top_k(biased.reshape(n_tok, N_EXPERT_GROUPS, -1), 2)[0].sum(-1)
    _, top_groups = lax.top_k(group_score, TOPK_GROUPS)
    group_mask = jnp.any(top_groups[..., None] == jnp.arange(N_EXPERT_GROUPS)[None, None, :], axis=1)
    expert_mask = jnp.repeat(group_mask, n_exp // N_EXPERT_GROUPS, axis=1)
    _, sel = lax.top_k(jnp.where(expert_mask, biased, -jnp.inf), TOP_K)
    w_sel = jnp.take_along_axis(scores, sel, axis=1)
    gates = w_sel / jnp.sum(w_sel, axis=-1, keepdims=True) * ROUTED_SCALE

    n_assign = n_tok * TOP_K
    e_flat = sel.reshape(-1).astype(jnp.int32)
    tok_flat = jnp.repeat(jnp.arange(n_tok, dtype=jnp.int32), TOP_K)
    order = jnp.argsort(e_flat)
    e_sorted = e_flat[order]
    counts = jnp.zeros((n_exp,), jnp.int32).at[e_flat].add(1)
    padded = (counts + MOE_BLOCK - 1) // MOE_BLOCK * MOE_BLOCK
    starts = jnp.cumsum(counts) - counts
    pad_ends = jnp.cumsum(padded)
    pad_starts = pad_ends - padded
    dest_sorted = pad_starts[e_sorted] + (jnp.arange(n_assign, dtype=jnp.int32) - starts[e_sorted])
    n_blocks = -(-(n_assign + n_exp * (MOE_BLOCK - 1)) // MOE_BLOCK)
    n_rows = n_blocks * MOE_BLOCK
    buf_tok = jnp.full((n_rows,), n_tok, jnp.int32).at[dest_sorted].set(tok_flat[order])
    dest = jnp.zeros((n_assign,), jnp.int32).at[order].set(dest_sorted).reshape(n_tok, TOP_K)
    block_start = jnp.arange(n_blocks, dtype=jnp.int32) * MOE_BLOCK
    block_expert = jnp.minimum(jnp.searchsorted(pad_ends, block_start, side='right'), n_exp - 1).astype(jnp.int32)
    block_active = (block_start < pad_ends[-1]).astype(jnp.int32)
    return gates, dest, buf_tok, block_expert, block_active


def _moe(stream, h, u, mod_l, w_router, b_router, wg, wu, wd, w_sg, w_su, w_sd):
    n, d = u.shape
    scores = _router_scores(stream, h, mod_l, w_router)
    gates, dest, buf_tok, block_expert, block_active = _route(scores, b_router)
    u_pad = jnp.concatenate([u, jnp.zeros((1, d), u.dtype)], axis=0)
    y_sorted = _expert_ffn(u_pad[buf_tok], block_expert, block_active, wg, wu, wd)
    routed = jnp.sum(y_sorted[dest].astype(F32) * gates[..., None], axis=1)
    shared = _shared_ffn(u, w_sg, w_su, w_sd)
    return routed, shared


def kernel(x_prompt, x_sample, c, state_gla, c_ctx, ln_in_g, ln_in_b, w_ada, b_ada, w_gla_in, w_gla_a1,
           w_gla_a2, b_gla_a, gla_norm_g, w_gla_out, w_fnet_out, ln_g, ln_b, w_router, b_router,
           w_exp_gate, w_exp_up, w_exp_down, w_sh_gate, w_sh_up, w_sh_down):
    n_ctx_seq, ctx_len, d = x_prompt.shape
    n_lat_seq, lat_len, _ = x_sample.shape
    depth = w_ada.shape[0]
    _, _, _, heads, dk, dv = state_gla.shape
    stream = _Stream(n_ctx_seq, ctx_len, n_lat_seq, lat_len)
    assert 1 + n_lat_seq <= 8
    alpha = (2 * depth) ** 0.25

    cond8 = jnp.zeros((8, d), F32).at[0].set(c_ctx).at[1:1 + n_lat_seq].set(c)
    mod = _ada_mod(cond8, w_ada, b_ada).reshape(depth, 8, N_MOD, 1, d)

    x_all = jnp.concatenate([x_prompt.reshape(-1, d), x_sample.reshape(-1, d)], axis=0)
    h, u = _embed(stream, x_all, ln_in_g, ln_in_b, mod[0])

    states = []
    for i in range(depth):
        j = i // 2
        if i % 2 == 0:
            mix, s_ctx = _gla_mixer(stream, u, state_gla, j, w_gla_in[j], w_gla_a1[j], w_gla_a2[j],
                                    b_gla_a[j], gla_norm_g[j], w_gla_out[j], heads, dk, dv)
            states.append(s_ctx)
        else:
            mix = _fnet_mixer(stream, u, w_fnet_out[j])
        h, u = _lnres(stream, h, [mix], mod[i], 2, ln_g[i, 0], ln_b[i, 0], alpha, mod[i], (3, 4))
        routed, shared = _moe(stream, h, u, mod[i], w_router[i], b_router[i],
                              w_exp_gate[i].astype(BF16), w_exp_up[i].astype(BF16),
                              w_exp_down[i].astype(BF16), w_sh_gate[i], w_sh_up[i], w_sh_down[i])
        last = i == depth - 1
        h, u = _lnres(stream, h, [routed, shared], mod[i], 5, ln_g[i, 1], ln_b[i, 1], alpha,
                      None if last else mod[i + 1], None if last else (0, 1))

    y_prompt = h[:stream.n_ctx].reshape(x_prompt.shape)
    y_sample = h[stream.n_ctx:].reshape(x_sample.shape)
    return y_prompt, y_sample, jnp.stack(states, axis=1)
```

```python
import functools
import math

import jax
import jax.numpy as jnp
from jax import lax
from jax.experimental import pallas as pl
from jax.experimental.pallas import tpu as pltpu

F32 = jnp.float32
BF16 = jnp.bfloat16

TOKEN_TILE = 256
V7X_VMEM_LIMIT_BYTES = 56 * 1024 * 1024
GRID_W = 64
GLA_TAU = 16.0
GLA_CHUNK = 64
FNET_GROUPS = 8
TOP_K = 8
N_EXPERT_GROUPS = 8
TOPK_GROUPS = 4
ROUTED_SCALE = 2.5
MOE_BLOCK = 256
LN_EPS = 1e-5
N_MOD = 6


def _params(n_axes):
    return pltpu.CompilerParams(dimension_semantics=("arbitrary",) * n_axes,
                                vmem_limit_bytes=V7X_VMEM_LIMIT_BYTES)


def _dot(a, b):
    return jnp.dot(a, b, preferred_element_type=F32)


def _silu(x):
    return x / (1.0 + jnp.exp(-x))


def _layer_norm(x, g, b):
    mu = jnp.mean(x, axis=-1, keepdims=True)
    xc = x - mu
    var = jnp.mean(xc * xc, axis=-1, keepdims=True)
    return xc * lax.rsqrt(var + LN_EPS) * g + b


class _Stream:
    def __init__(self, n_ctx_seq, ctx_len, n_lat_seq, lat_len):
        assert ctx_len % TOKEN_TILE == 0 and lat_len % TOKEN_TILE == 0
        self.n_ctx_seq, self.ctx_len = n_ctx_seq, ctx_len
        self.n_lat_seq, self.lat_len = n_lat_seq, lat_len
        self.n_ctx = n_ctx_seq * ctx_len
        self.n_tok = self.n_ctx + n_lat_seq * lat_len
        self.ctx_tiles = self.n_ctx // TOKEN_TILE
        self.lat_seq_tiles = lat_len // TOKEN_TILE
        self.n_tiles = self.n_tok // TOKEN_TILE

    def mod_row(self, t):
        return jnp.where(t < self.ctx_tiles, 0, 1 + (t - self.ctx_tiles) // self.lat_seq_tiles)


def _mod_spec(stream, d, col, tile_rows=TOKEN_TILE):
    return pl.BlockSpec((None, None, 1, d),
                        lambda t, *_: (stream.mod_row((t * tile_rows) // TOKEN_TILE), col, 0, 0))


def _pack_bf16_pairs(y):
    half = y.shape[1] // 2
    lo = lax.bitcast_convert_type(y[:, :half].astype(BF16).astype(F32), jnp.uint32) >> 16
    hi = lax.bitcast_convert_type(y[:, half:].astype(BF16).astype(F32), jnp.uint32) & jnp.uint32(0xFFFF0000)
    return lo | hi


def _unpack_bf16_pairs(w):
    lo = lax.bitcast_convert_type(w << 16, F32)
    hi = lax.bitcast_convert_type(w & jnp.uint32(0xFFFF0000), F32)
    return lo, hi


def _row_spec(d):
    return pl.BlockSpec((1, d), lambda t, *_: (0, 0))


def _ada_kernel(c_ref, w_ref, b_ref, o_ref):
    s = _silu(c_ref[...]).astype(BF16)
    o_ref[...] = _dot(s, w_ref[...].astype(BF16)) + b_ref[...]


def _ada_mod(cond8, w_ada, b_ada):
    depth, d, n_out = w_ada.shape
    tn = min(512, n_out)
    return pl.pallas_call(
        _ada_kernel,
        grid=(depth, n_out // tn),
        in_specs=[pl.BlockSpec((8, d), lambda l, j: (0, 0)),
                  pl.BlockSpec((None, d, tn), lambda l, j: (l, 0, j)),
                  pl.BlockSpec((None, 1, tn), lambda l, j: (l, 0, j))],
        out_specs=pl.BlockSpec((None, 8, tn), lambda l, j: (l, 0, j)),
        out_shape=jax.ShapeDtypeStruct((depth, 8, n_out), F32),
        compiler_params=_params(2),
        name="ada_mod",
    )(cond8, w_ada, b_ada.reshape(depth, 1, n_out))


def _embed_kernel(xp_ref, xs_ref, g_ref, b_ref, sc_ref, sh_ref, h_ref, u_ref, tab_ref, *, stream):
    t = pl.program_id(0)
    d = xp_ref.shape[1]
    quarter = d // 4
    half = 2 * quarter

    @pl.when(t == 0)
    def _():
        j = lax.broadcasted_iota(jnp.int32, (GRID_W, quarter), 1).astype(F32)
        p = lax.broadcasted_iota(jnp.int32, (GRID_W, quarter), 0).astype(F32)
        ang = p * jnp.exp((-math.log(10000.0) * j) / quarter)
        tab_ref[:, :quarter] = jnp.sin(ang)
        tab_ref[:, quarter:] = jnp.cos(ang)

    is_lat = t >= stream.ctx_tiles
    seq_tile = jnp.maximum(t - stream.ctx_tiles, 0) % stream.lat_seq_tiles
    rows_per_tile = TOKEN_TILE // GRID_W
    col_emb = tab_ref[...]
    for j in range(rows_per_tile):
        row_emb = tab_ref[pl.ds(seq_tile * rows_per_tile + j, 1), :]
        pos = jnp.concatenate([jnp.broadcast_to(row_emb, (GRID_W, half)), col_emb], axis=1)
        rows = slice(j * GRID_W, (j + 1) * GRID_W)
        x = jnp.where(is_lat, xs_ref[rows, :] + pos, xp_ref[rows, :])
        y = _layer_norm(x, g_ref[...], b_ref[...])
        h_ref[rows, :] = y
        u_ref[rows, :] = (y * (1.0 + sc_ref[...]) + sh_ref[...]).astype(BF16)


def _embed(stream, x_ctx, x_lat, ln_g, ln_b, mod_l):
    d = x_ctx.shape[1]
    n = stream.n_tok
    assert stream.lat_len // GRID_W <= GRID_W
    tile = pl.BlockSpec((TOKEN_TILE, d), lambda t: (t, 0))
    ctx_tile = pl.BlockSpec((TOKEN_TILE, d), lambda t: (jnp.minimum(t, stream.ctx_tiles - 1), 0))
    lat_tile = pl.BlockSpec((TOKEN_TILE, d), lambda t: (jnp.maximum(t - stream.ctx_tiles, 0), 0))
    return pl.pallas_call(
        functools.partial(_embed_kernel, stream=stream),
        grid=(stream.n_tiles,),
        in_specs=[ctx_tile, lat_tile, _row_spec(d), _row_spec(d),
                  _mod_spec(stream, d, 1), _mod_spec(stream, d, 0)],
        out_specs=[tile, tile],
        out_shape=[jax.ShapeDtypeStruct((n, d), F32), jax.ShapeDtypeStruct((n, d), BF16)],
        scratch_shapes=[pltpu.VMEM((GRID_W, d // 2), F32)],
        compiler_params=_params(1),
        name="embed_ln",
    )(x_ctx, x_lat, ln_g.reshape(1, d), ln_b.reshape(1, d), mod_l, mod_l)


def _lnres_kernel(*refs, n_add, has_next, alpha, packed):
    h_ref = refs[0]
    add_refs = refs[1:1 + n_add]
    gate_ref, g_ref, b_ref = refs[1 + n_add:4 + n_add]
    rest = refs[4 + n_add:]
    a = add_refs[0][...].astype(F32)
    for r in add_refs[1:]:
        a = a + r[...].astype(F32)
    y = _layer_norm(alpha * h_ref[...] + gate_ref[...] * a, g_ref[...], b_ref[...])
    if has_next:
        sc_ref, sh_ref, hn_ref, u_ref = rest
        u = y * (1.0 + sc_ref[...]) + sh_ref[...]
        u_ref[...] = _pack_bf16_pairs(u) if packed else u.astype(BF16)
    else:
        (hn_ref,) = rest
    hn_ref[...] = y


def _lnres(stream, h, adds, mod_l, gate_col, ln_g, ln_b, alpha, next_mod=None, next_cols=None,
           packed=False):
    n, d = h.shape
    tile = pl.BlockSpec((TOKEN_TILE, d), lambda t: (t, 0))
    has_next = next_mod is not None
    in_specs = [tile] + [tile] * len(adds) + [_mod_spec(stream, d, gate_col), _row_spec(d), _row_spec(d)]
    args = [h, *adds, mod_l, ln_g.reshape(1, d), ln_b.reshape(1, d)]
    out_specs = [tile]
    out_shape = [jax.ShapeDtypeStruct((n, d), F32)]
    if has_next:
        in_specs += [_mod_spec(stream, d, next_cols[1]), _mod_spec(stream, d, next_cols[0])]
        args += [next_mod, next_mod]
        if packed:
            out_specs.append(pl.BlockSpec((TOKEN_TILE, d // 2), lambda t: (t, 0)))
            out_shape.append(jax.ShapeDtypeStruct((n, d // 2), jnp.uint32))
        else:
            out_specs.append(tile)
            out_shape.append(jax.ShapeDtypeStruct((n, d), BF16))
    out = pl.pallas_call(
        functools.partial(_lnres_kernel, n_add=len(adds), has_next=has_next, alpha=alpha, packed=packed),
        grid=(stream.n_tiles,),
        in_specs=in_specs, out_specs=out_specs, out_shape=out_shape,
        compiler_params=_params(1),
        name="residual_ln",
    )(*args)
    return (out[0], out[1]) if has_next else (out[0], None)


def _mm_kernel(*refs, split_at):
    if split_at is None:
        x_ref, w_ref, o_ref, wbf_ref = refs
        x = x_ref[...]
    else:
        xa_ref, xb_ref, w_ref, o_ref, wbf_ref = refs
        x = jnp.where(pl.program_id(1) < split_at, xa_ref[...], xb_ref[...])

    @pl.when(pl.program_id(1) == 0)
    def _():
        wbf_ref[...] = w_ref[...].astype(BF16)

    o_ref[...] = _dot(x, wbf_ref[...]).astype(o_ref.dtype)


def _matmul(x, w, out_dtype, layer=None, tm=512, tn=512):
    parts = x if isinstance(x, (tuple, list)) else (x,)
    n = sum(p.shape[0] for p in parts)
    k = parts[0].shape[1]
    n_out = w.shape[-1]
    tm, tn = min(tm, n), min(tn, n_out)
    assert n % tm == 0 and n_out % tn == 0
    if layer is None:
        w_spec = pl.BlockSpec((k, tn), lambda j, i: (0, j))
    else:
        w_spec = pl.BlockSpec((None, k, tn), lambda j, i: (layer, 0, j))
    if len(parts) == 1:
        split_at = None
        x_specs = [pl.BlockSpec((tm, k), lambda j, i: (i, 0))]
    else:
        assert parts[0].shape[0] % tm == 0
        split_at = parts[0].shape[0] // tm
        x_specs = [pl.BlockSpec((tm, k), lambda j, i: (jnp.minimum(i, split_at - 1), 0)),
                   pl.BlockSpec((tm, k), lambda j, i: (jnp.maximum(i - split_at, 0), 0))]
    return pl.pallas_call(
        functools.partial(_mm_kernel, split_at=split_at),
        grid=(n_out // tn, n // tm),
        in_specs=x_specs + [w_spec],
        out_specs=pl.BlockSpec((tm, tn), lambda j, i: (i, j)),
        out_shape=jax.ShapeDtypeStruct((n, n_out), out_dtype),
        scratch_shapes=[pltpu.VMEM((k, tn), BF16)],
        compiler_params=_params(2),
        name="matmul",
    )(*parts, w)


def _gla_kernel(q_ref, k_ref, v_ref, r_ref, z_ref, w2_ref, ba_ref, gn_ref, *rest,
                nb, has_s0, write_state, state_layer, scale):
    rest = list(rest)
    s0_ref = rest.pop(0) if has_s0 else None
    o_ref = rest.pop(0)
    sf_ref = rest.pop(0) if write_state else None
    state_ref, ofwd_ref = rest
    t = pl.program_id(2)
    tm = q_ref.shape[0]
    c = GLA_CHUNK

    @pl.when((t == 0) | (t == nb))
    def _():
        state_ref[...] = s0_ref[...] if has_s0 else jnp.zeros_like(state_ref)

    ii = lax.broadcasted_iota(jnp.int32, (c, c), 0)
    jj = lax.broadcasted_iota(jnp.int32, (c, c), 1)

    def run(bwd):
        blk = (2 * nb - 1 - t) if bwd else t
        mask = (jj >= ii) if bwd else (jj <= ii)
        tri = mask.astype(BF16)
        chunks = range(tm // c)
        for ci in (reversed(chunks) if bwd else chunks):
            rows = slice(ci * c, (ci + 1) * c)
            g = _dot(z_ref[rows, :].astype(BF16), w2_ref[...]) + ba_ref[...]
            log_a = (jnp.minimum(g, 0.0) - jnp.log(1.0 + jnp.exp(-jnp.abs(g)))) * (1.0 / GLA_TAU)
            hi = log_a.astype(BF16)
            r1 = log_a - hi.astype(F32)
            mid = r1.astype(BF16)
            lo = (r1 - mid.astype(F32)).astype(BF16)
            cum = _dot(tri, hi) + _dot(tri, mid) + _dot(tri, lo)
            total = cum[0:1, :] if bwd else cum[c - 1:c, :]
            q = q_ref[rows, :].astype(F32)
            k = k_ref[rows, :].astype(F32)
            v = v_ref[rows, :]
            q_dec = (q * jnp.exp(cum) * scale).astype(BF16)
            k_inv = (k * jnp.exp(-cum)).astype(BF16)
            k_end = (k * jnp.exp(total - cum)).astype(BF16)
            scores = lax.dot_general(q_dec, k_inv, (((1,), (1,)), ((), ())), preferred_element_type=F32)
            scores = jnp.where(mask, scores, 0.0).astype(BF16)
            state = state_ref[...]
            o = _dot(scores, v) + _dot(q_dec, state.astype(BF16))
            kv = lax.dot_general(k_end, v, (((0,), (0,)), ((), ())), preferred_element_type=F32)
            state_ref[...] = state * jnp.transpose(jnp.exp(total)) + kv
            seq_rows = pl.ds(pl.multiple_of(blk * tm + ci * c, c), c)
            if not bwd:
                ofwd_ref[seq_rows, :] = o
            else:
                o = ofwd_ref[seq_rows, :] + o
                mu = jnp.mean(o, axis=-1, keepdims=True)
                oc = o - mu
                var = jnp.mean(oc * oc, axis=-1, keepdims=True)
                y = oc * lax.rsqrt(var + LN_EPS) * gn_ref[...]
                o_ref[rows, :] = (y * _silu(r_ref[rows, :].astype(F32))).astype(BF16)

    pl.when(t < nb)(lambda: run(False))
    pl.when(t >= nb)(lambda: run(True))

    if write_state:
        @pl.when((t == nb - 1) | (t == 2 * nb - 1))
        def _():
            if state_layer is None:
                sf_ref[...] = state_ref[...]
            else:
                for l in range(sf_ref.shape[0]):
                    sf_ref[l] = state_ref[...] if l == state_layer else jnp.zeros_like(state_ref)


def _gla_scan(proj, z, w2ext, b_a, g_norm, s0, st_prev, *, n_seq, seq_len, tile0, heads, dk, dv,
              layer, n_layers, write_state):
    tm = TOKEN_TILE
    nb = seq_len // tm
    k_col0, v_col0, r_col0 = heads, (2 * heads * dk) // dv, (2 * heads * dk + heads * dv) // dv
    assert (2 * heads * dk) % dv == 0

    def blk(t):
        return jnp.where(t < nb, t, 2 * nb - 1 - t)

    def tok(b, t):
        return tile0 + b * nb + blk(t)

    def out_tok(b, t):
        return b * nb + jnp.where(t < nb, nb - 1, 2 * nb - 1 - t)

    in_specs = [
        pl.BlockSpec((tm, dk), lambda b, h, t: (tok(b, t), h)),
        pl.BlockSpec((tm, dk), lambda b, h, t: (tok(b, t), k_col0 + h)),
        pl.BlockSpec((tm, dv), lambda b, h, t: (tok(b, t), v_col0 + h)),
        pl.BlockSpec((tm, dv), lambda b, h, t: (tok(b, t), r_col0 + h)),
        pl.BlockSpec((tm, 128), lambda b, h, t: (tok(b, t), 0)),
        pl.BlockSpec((None, 128, dk), lambda b, h, t: (t // nb, 0, h)),
        pl.BlockSpec((None, 1, dk), lambda b, h, t: (t // nb, 0, h)),
        pl.BlockSpec((1, dv), lambda b, h, t: (0, h)),
    ]
    args = [proj, proj, proj, proj, z, w2ext, b_a.reshape(2, 1, heads * dk), g_norm.reshape(1, heads * dv)]
    if s0 is not None:
        in_specs.append(pl.BlockSpec((None, None, None, None, dk, dv),
                                     lambda b, h, t: (b, layer, t // nb, h, 0, 0)))
        args.append(s0)
    n_compute_args = len(args)
    aliases = {}
    if st_prev is not None:
        in_specs.append(pl.BlockSpec(memory_space=pl.ANY))
        args.append(st_prev)
        aliases[len(args) - 1] = 1
    n_alias_args = len(args) - n_compute_args
    out_specs = [pl.BlockSpec((tm, dv), lambda b, h, t: (out_tok(b, t), h))]
    out_shape = [jax.ShapeDtypeStruct((n_seq * seq_len, heads * dv), BF16)]
    state_layer = None
    if write_state:
        if st_prev is None:
            state_layer = layer
            out_specs.append(pl.BlockSpec((None, n_layers, None, None, dk, dv),
                                          lambda b, h, t: (b, 0, t // nb, h, 0, 0)))
        else:
            out_specs.append(pl.BlockSpec((None, None, None, None, dk, dv),
                                          lambda b, h, t: (b, layer, t // nb, h, 0, 0)))
        out_shape.append(jax.ShapeDtypeStruct((n_seq, n_layers, 2, heads, dk, dv), F32))

    def body(*refs):
        refs = list(refs)
        del refs[n_compute_args:n_compute_args + n_alias_args]
        _gla_kernel(*refs, nb=nb, has_s0=s0 is not None, write_state=write_state,
                    state_layer=state_layer, scale=dk ** -0.5)

    out = pl.pallas_call(
        body,
        grid=(n_seq, heads, 2 * nb),
        in_specs=in_specs, out_specs=out_specs, out_shape=out_shape,
        scratch_shapes=[pltpu.VMEM((dk, dv), F32), pltpu.VMEM((seq_len, dv), F32)],
        input_output_aliases=aliases,
        compiler_params=_params(3),
        name="gla_scan",
    )(*args)
    return out


def _gla_mixer(stream, u, s0_lat, st_prev, layer, n_layers, w_in, w_a1, w_a2, b_a, g_norm, w_out,
               heads, dk, dv):
    d = u.shape[1]
    rank = w_a1.shape[-1]
    assert 2 * rank <= 128
    proj = _matmul(u, w_in, BF16, layer=layer)
    w_a1_cat = jnp.zeros((d, 128), F32).at[:, :rank].set(w_a1[0]).at[:, rank:2 * rank].set(w_a1[1])
    z = _matmul(u, w_a1_cat, F32)
    w2ext = jnp.zeros((2, 128, heads * dk), F32)
    w2ext = w2ext.at[0, :rank].set(w_a2[0]).at[1, rank:2 * rank].set(w_a2[1]).astype(BF16)
    common = dict(heads=heads, dk=dk, dv=dv, layer=layer, n_layers=n_layers)
    o_ctx, s_ctx = _gla_scan(proj, z, w2ext, b_a, g_norm, None, st_prev, n_seq=stream.n_ctx_seq,
                             seq_len=stream.ctx_len, tile0=0, write_state=True, **common)
    (o_lat,) = _gla_scan(proj, z, w2ext, b_a, g_norm, s0_lat, None, n_seq=stream.n_lat_seq,
                         seq_len=stream.lat_len, tile0=stream.ctx_tiles, write_state=False, **common)
    return _matmul((o_ctx, o_lat), w_out, BF16, layer=layer), s_ctx


def _dft_tables(n, scale):
    idx = jnp.arange(n, dtype=jnp.int32)
    ang = ((idx[:, None] * idx[None, :]) % n).astype(F32) * (2.0 * math.pi / n)
    return jnp.cos(ang) * scale, jnp.sin(ang) * scale


def _fnet_chan_kernel(x_ref, t_ref, a_ref, b_ref):
    w = t_ref.shape[0]
    for g in range(FNET_GROUPS):
        cols = slice(g * w, (g + 1) * w)
        ab = _dot(x_ref[:, cols], t_ref[...])
        a_ref[:, cols] = ab[:, :w].astype(BF16)
        b_ref[:, cols] = ab[:, w:].astype(BF16)


def _fnet_chan(u, table):
    n, d = u.shape
    tm = min(512, n)
    tile = pl.BlockSpec((tm, d), lambda i: (i, 0))
    return pl.pallas_call(
        _fnet_chan_kernel,
        grid=(n // tm,),
        in_specs=[tile, pl.BlockSpec(table.shape, lambda i: (0, 0))],
        out_specs=[tile, tile],
        out_shape=[jax.ShapeDtypeStruct((n, d), BF16)] * 2,
        compiler_params=_params(1),
        name="fnet_channel_dft",
    )(u, table)


def _fnet_seq_kernel(tc_ref, ts_ref, a_ref, b_ref, o_ref):
    o_ref[...] = (_dot(tc_ref[...], a_ref[...]) + _dot(ts_ref[...], b_ref[...])).astype(BF16)


def _fnet_seq(a, b, tc, ts, *, n_seq, seq_len, row0):
    d = a.shape[1]
    tm = min(512, seq_len)
    tn = min(512, d)
    assert row0 % seq_len == 0
    seq0 = row0 // seq_len
    nrb = seq_len // tm
    return pl.pallas_call(
        _fnet_seq_kernel,
        grid=(n_seq, d // tn, nrb),
        in_specs=[pl.BlockSpec((tm, seq_len), lambda s, j, i: (i, 0)),
                  pl.BlockSpec((tm, seq_len), lambda s, j, i: (i, 0)),
                  pl.BlockSpec((seq_len, tn), lambda s, j, i: (seq0 + s, j)),
                  pl.BlockSpec((seq_len, tn), lambda s, j, i: (seq0 + s, j))],
        out_specs=pl.BlockSpec((tm, tn), lambda s, j, i: (s * nrb + i, j)),
        out_shape=jax.ShapeDtypeStruct((n_seq * seq_len, d), BF16),
        compiler_params=_params(3),
        name="fnet_sequence_dft",
    )(tc, ts, a, b)


def _fnet_mixer(stream, u, w_out, layer):
    d = u.shape[1]
    w = d // FNET_GROUPS
    cw, sw = _dft_tables(w, w ** -0.5)
    a, b = _fnet_chan(u, jnp.concatenate([cw, sw], axis=1).astype(BF16))
    mixed = []
    for n_seq, seq_len, row0 in ((stream.n_ctx_seq, stream.ctx_len, 0),
                                 (stream.n_lat_seq, stream.lat_len, stream.n_ctx)):
        cl, sl = _dft_tables(seq_len, seq_len ** -0.5)
        mixed.append(_fnet_seq(a, b, cl.astype(BF16), (-sl).astype(BF16),
                               n_seq=n_seq, seq_len=seq_len, row0=row0))
    return _matmul(mixed, w_out, BF16, layer=layer)


def _route_kernel(h_ref, sc_ref, sh_ref, whi_ref, wlo_ref, br_ref, e_ref, rank_ref, gate_ref, cnt_ref,
                  base_ref):
    t = pl.program_id(0)
    n_exp = whi_ref.shape[0]
    tm = h_ref.shape[0]
    gsz = n_exp // N_EXPERT_GROUPS

    @pl.when(t == 0)
    def _():
        base_ref[...] = jnp.zeros_like(base_ref)

    u = h_ref[...] * (1.0 + sc_ref[...]) + sh_ref[...]
    u_hi = u.astype(BF16)
    u_lo = (u - u_hi.astype(F32)).astype(BF16)
    nt = (((1,), (1,)), ((), ()))
    logits = (lax.dot_general(whi_ref[...], u_hi, nt, preferred_element_type=F32)
              + lax.dot_general(wlo_ref[...], u_hi, nt, preferred_element_type=F32)
              + lax.dot_general(whi_ref[...], u_lo, nt, preferred_element_type=F32))
    scores = 1.0 / (1.0 + jnp.exp(-logits))
    biased = scores + br_ref[...]

    sub = lax.broadcasted_iota(jnp.int32, (gsz, tm), 0).astype(F32)
    gidx = lax.broadcasted_iota(jnp.int32, (N_EXPERT_GROUPS, tm), 0)
    gs = jnp.zeros((N_EXPERT_GROUPS, tm), F32)
    for g in range(N_EXPERT_GROUPS):
        x = biased[g * gsz:(g + 1) * gsz, :]
        m1 = jnp.max(x, axis=0, keepdims=True)
        first = jnp.min(jnp.where(x == m1, sub, float(gsz)), axis=0, keepdims=True)
        m2 = jnp.max(jnp.where(sub == first, -jnp.inf, x), axis=0, keepdims=True)
        gs = jnp.where(gidx == g, m1 + m2, gs)
    ahead = jnp.zeros((N_EXPERT_GROUPS, tm), F32)
    for g2 in range(N_EXPERT_GROUPS):
        row = gs[g2:g2 + 1, :]
        ahead = ahead + jnp.where((row > gs) | ((row == gs) & (gidx > g2)), 1.0, 0.0)
    keep = jnp.where(ahead < TOPK_GROUPS, 1.0, 0.0)
    keep_e = jnp.concatenate([jnp.broadcast_to(keep[g:g + 1, :], (gsz, tm))
                              for g in range(N_EXPERT_GROUPS)], axis=0)
    masked = jnp.where(keep_e > 0.5, biased, -jnp.inf)

    eidx = lax.broadcasted_iota(jnp.int32, (n_exp, tm), 0)
    ahead = jnp.zeros((n_exp, tm), F32)
    for e2 in range(n_exp):
        row = masked[e2:e2 + 1, :]
        ahead = ahead + jnp.where((row > masked) | ((row == masked) & (eidx > e2)), 1.0, 0.0)
    sel = jnp.where(ahead < TOP_K, 1.0, 0.0)

    w_sel = scores * sel
    gates = w_sel / jnp.sum(w_sel, axis=0, keepdims=True) * ROUTED_SCALE

    sel_bf = sel.astype(BF16)
    ti = lax.broadcasted_iota(jnp.int32, (tm, tm), 0)
    tj = lax.broadcasted_iota(jnp.int32, (tm, tm), 1)
    pos = _dot(sel_bf, (ti < tj).astype(BF16))
    ei = lax.broadcasted_iota(jnp.int32, (n_exp, n_exp), 0)
    ej = lax.broadcasted_iota(jnp.int32, (n_exp, n_exp), 1)
    slot = _dot((ej < ei).astype(BF16), sel_bf)
    rank = base_ref[:, 0:1] + pos
    base_ref[...] = base_ref[...] + jnp.sum(sel, axis=1, keepdims=True)
    cnt_ref[...] = base_ref[...]

    kidx = lax.broadcasted_iota(jnp.int32, (TOP_K, tm), 0)
    eidx_f = eidx.astype(F32)
    e_out = jnp.zeros((TOP_K, tm), F32)
    r_out = jnp.zeros((TOP_K, tm), F32)
    g_out = jnp.zeros((TOP_K, tm), F32)
    for k in range(TOP_K):
        pick = jnp.where(slot == float(k), sel, 0.0)
        e_out = jnp.where(kidx == k, jnp.sum(pick * eidx_f, axis=0, keepdims=True), e_out)
        r_out = jnp.where(kidx == k, jnp.sum(pick * rank, axis=0, keepdims=True), r_out)
        g_out = jnp.where(kidx == k, jnp.sum(pick * gates, axis=0, keepdims=True), g_out)
    e_ref[...] = e_out.astype(jnp.int32)
    rank_ref[...] = r_out.astype(jnp.int32)
    gate_ref[...] = g_out


def _route(stream, h, mod_l, w_router, b_router):
    n, d = h.shape
    n_exp = w_router.shape[1]
    assert n_exp % N_EXPERT_GROUPS == 0 and (n_exp // N_EXPERT_GROUPS) % 8 == 0
    w_t = w_router.T
    w_hi = w_t.astype(BF16)
    w_lo = (w_t - w_hi.astype(F32)).astype(BF16)
    wspec = pl.BlockSpec((n_exp, d), lambda t: (0, 0))
    per_tok = pl.BlockSpec((TOP_K, TOKEN_TILE), lambda t: (0, t))
    return pl.pallas_call(
        _route_kernel,
        grid=(stream.n_tiles,),
        in_specs=[pl.BlockSpec((TOKEN_TILE, d), lambda t: (t, 0)),
                  _mod_spec(stream, d, 4), _mod_spec(stream, d, 3), wspec, wspec,
                  pl.BlockSpec((n_exp, 1), lambda t: (0, 0))],
        out_specs=[per_tok, per_tok, per_tok, pl.BlockSpec((n_exp, 128), lambda t: (0, 0))],
        out_shape=[jax.ShapeDtypeStruct((TOP_K, n), jnp.int32), jax.ShapeDtypeStruct((TOP_K, n), jnp.int32),
                   jax.ShapeDtypeStruct((TOP_K, n), F32), jax.ShapeDtypeStruct((n_exp, 128), F32)],
        scratch_shapes=[pltpu.VMEM((n_exp, 128), F32)],
        compiler_params=_params(1),
        name="moe_route",
    )(h, mod_l, mod_l, w_hi, w_lo, b_router.astype(F32).reshape(n_exp, 1))


def _moe_tables(e_t, rank_t, counts_f, n_assign):
    n_exp = counts_f.shape[0]
    counts = counts_f.astype(jnp.int32)
    padded = (counts + MOE_BLOCK - 1) // MOE_BLOCK * MOE_BLOCK
    pad_ends = jnp.cumsum(padded)
    pad_starts = pad_ends - padded
    n_blocks = -(-(n_assign + n_exp * (MOE_BLOCK - 1)) // MOE_BLOCK)
    block_start = jnp.arange(n_blocks, dtype=jnp.int32) * MOE_BLOCK
    block_expert = jnp.minimum(jnp.sum((pad_ends[None, :] <= block_start[:, None]).astype(jnp.int32), axis=1),
                               n_exp - 1).astype(jnp.int32)
    block_active = (block_start < pad_ends[-1]).astype(jnp.int32)
    onehot = e_t[None] == jnp.arange(n_exp, dtype=jnp.int32)[:, None, None]
    dest = jnp.sum(jnp.where(onehot, pad_starts[:, None, None], 0), axis=0) + rank_t
    fill_start = jnp.concatenate([pad_starts + counts, pad_ends[-1:]])
    fill_count = jnp.concatenate([padded - counts, n_blocks * MOE_BLOCK - pad_ends[-1:]])
    return dest, fill_start, fill_count, block_expert, block_active, n_blocks


def _dispatch_kernel(fill0_ref, filln_ref, dest_hbm, u_ref, xs_ref, idx_ref, zero_ref,
                     sem_idx, sem_row, sem_zero):
    t = pl.program_id(0)
    tm = u_ref.shape[0]
    n_exp = fill0_ref.shape[0]
    idx_copy = pltpu.make_async_copy(dest_hbm.at[t], idx_ref, sem_idx)
    idx_copy.start()

    def zero_copy(dst):
        return pltpu.make_async_copy(zero_ref.at[pl.ds(0, 1), :], xs_ref.at[pl.ds(dst, 1), :], sem_zero)

    def row_copy(row, dst):
        return pltpu.make_async_copy(u_ref.at[pl.ds(row, 1), :], xs_ref.at[pl.ds(dst, 1), :], sem_row)

    @pl.when(t == 0)
    def _():
        zero_ref[...] = jnp.zeros_like(zero_ref)
        for e in range(n_exp):
            lax.fori_loop(0, filln_ref[e], lambda r, c, e=e: (zero_copy(fill0_ref[e] + r).start(), c)[1], 0)

    idx_copy.wait()
    lanes = idx_ref.shape[1]
    for r in range(idx_ref.shape[0]):
        row0 = (r % (tm // lanes)) * lanes

        def issue(c, carry, r=r, row0=row0):
            row_copy(row0 + c, idx_ref[r, c]).start()
            return carry

        lax.fori_loop(0, lanes, issue, 0, unroll=8)

    def drain(i, carry):
        row_copy(0, 0).wait()
        return carry

    lax.fori_loop(0, tm * TOP_K, drain, 0, unroll=16)

    @pl.when(t == 0)
    def _():
        for e in range(n_exp):
            lax.fori_loop(0, filln_ref[e], lambda r, c: (zero_copy(0).wait(), c)[1], 0)


def _dispatch(stream, u_packed, dest, fill_start, fill_count, n_rows):
    n, d2 = u_packed.shape
    tm = TOKEN_TILE
    lanes = 128
    groups = tm // lanes
    dest_tiles = dest.reshape(TOP_K, stream.n_tiles, groups, lanes).transpose(1, 0, 2, 3)
    dest_tiles = dest_tiles.reshape(stream.n_tiles, TOP_K * groups, lanes)
    grid_spec = pltpu.PrefetchScalarGridSpec(
        num_scalar_prefetch=2,
        grid=(stream.n_tiles,),
        in_specs=[pl.BlockSpec(memory_space=pl.ANY),
                  pl.BlockSpec((tm, d2), lambda t, f0, fn: (t, 0))],
        out_specs=pl.BlockSpec(memory_space=pl.ANY),
        scratch_shapes=[pltpu.SMEM((TOP_K * groups, lanes), jnp.int32),
                        pltpu.VMEM((8, d2), jnp.uint32),
                        pltpu.SemaphoreType.DMA, pltpu.SemaphoreType.DMA, pltpu.SemaphoreType.DMA],
    )
    return pl.pallas_call(
        _dispatch_kernel,
        grid_spec=grid_spec,
        out_shape=jax.ShapeDtypeStruct((n_rows, d2), jnp.uint32),
        compiler_params=_params(1),
        name="moe_dispatch",
    )(fill_start, fill_count, dest_tiles, u_packed)


def _packed_dot(lo, hi, w_ref):
    half = lo.shape[1]
    return _dot(lo, w_ref[:half, :]) + _dot(hi, w_ref[half:, :])


def _expert_changed(be_ref, b):
    return (b == 0) | (be_ref[b] != be_ref[jnp.maximum(b - 1, 0)])


def _expert_up_kernel(be_ref, act_ref, x_ref, wg_ref, wu_ref, o_ref, wg_bf, wu_bf):
    b = pl.program_id(0)

    @pl.when(_expert_changed(be_ref, b))
    def _():
        wg_bf[...] = wg_ref[...].astype(BF16)
        wu_bf[...] = wu_ref[...].astype(BF16)

    @pl.when(act_ref[b] != 0)
    def _():
        lo, hi = _unpack_bf16_pairs(x_ref[...])
        lo, hi = lo.astype(BF16), hi.astype(BF16)
        o_ref[...] = (_silu(_packed_dot(lo, hi, wg_bf)) * _packed_dot(lo, hi, wu_bf)).astype(BF16)

    @pl.when(act_ref[b] == 0)
    def _():
        o_ref[...] = jnp.zeros_like(o_ref)


def _expert_down_kernel(be_ref, act_ref, h_ref, wd_ref, o_ref, wd_bf):
    b = pl.program_id(0)

    @pl.when(_expert_changed(be_ref, b))
    def _():
        wd_bf[...] = wd_ref[...].astype(BF16)

    @pl.when(act_ref[b] != 0)
    def _():
        o_ref[...] = _pack_bf16_pairs(_dot(h_ref[...], wd_bf[...]))

    @pl.when(act_ref[b] == 0)
    def _():
        o_ref[...] = jnp.zeros_like(o_ref)


def _expert_ffn(x_sorted, block_expert, block_active, n_blocks, wg, wu, wd, layer):
    d2 = x_sorted.shape[1]
    _, _, d, f = wg.shape

    def w_spec(shape):
        return pl.BlockSpec((None, None) + shape, lambda b, be, act: (layer, be[b], 0, 0))

    hidden = pl.pallas_call(
        _expert_up_kernel,
        grid_spec=pltpu.PrefetchScalarGridSpec(
            num_scalar_prefetch=2,
            grid=(n_blocks,),
            in_specs=[pl.BlockSpec((MOE_BLOCK, d2), lambda b, be, act: (b, 0)), w_spec((d, f)), w_spec((d, f))],
            out_specs=pl.BlockSpec((MOE_BLOCK, f), lambda b, be, act: (b, 0)),
            scratch_shapes=[pltpu.VMEM((d, f), BF16), pltpu.VMEM((d, f), BF16)]),
        out_shape=jax.ShapeDtypeStruct((n_blocks * MOE_BLOCK, f), BF16),
        compiler_params=_params(1),
        name="moe_expert_up",
    )(block_expert, block_active, x_sorted, wg, wu)
    return pl.pallas_call(
        _expert_down_kernel,
        grid_spec=pltpu.PrefetchScalarGridSpec(
            num_scalar_prefetch=2,
            grid=(n_blocks,),
            in_specs=[pl.BlockSpec((MOE_BLOCK, f), lambda b, be, act: (b, 0)), w_spec((f, d))],
            out_specs=pl.BlockSpec((MOE_BLOCK, d2), lambda b, be, act: (b, 0)),
            scratch_shapes=[pltpu.VMEM((f, d), BF16)]),
        out_shape=jax.ShapeDtypeStruct((n_blocks * MOE_BLOCK, d2), jnp.uint32),
        compiler_params=_params(1),
        name="moe_expert_down",
    )(block_expert, block_active, hidden, wd)


def _shared_kernel(x_ref, wg_ref, wu_ref, wd_ref, o_ref):
    lo, hi = _unpack_bf16_pairs(x_ref[...])
    lo, hi = lo.astype(BF16), hi.astype(BF16)
    hidden = (_silu(_packed_dot(lo, hi, wg_ref)) * _packed_dot(lo, hi, wu_ref)).astype(BF16)
    o_ref[...] = _dot(hidden, wd_ref[...]).astype(BF16)


def _shared_ffn(u_packed, wg, wu, wd):
    n, d2 = u_packed.shape
    d, f = wg.shape
    tm = min(512, n)
    return pl.pallas_call(
        _shared_kernel,
        grid=(n // tm,),
        in_specs=[pl.BlockSpec((tm, d2), lambda i: (i, 0)),
                  pl.BlockSpec((d, f), lambda i: (0, 0)), pl.BlockSpec((d, f), lambda i: (0, 0)),
                  pl.BlockSpec((f, d), lambda i: (0, 0))],
        out_specs=pl.BlockSpec((tm, d), lambda i: (i, 0)),
        out_shape=jax.ShapeDtypeStruct((n, d), BF16),
        compiler_params=_params(1),
        name="moe_shared_expert",
    )(u_packed, wg.astype(BF16), wu.astype(BF16), wd.astype(BF16))


COMBINE_TILE = 128


def _combine_kernel(*refs, alpha, has_next, split_at):
    dest_hbm, ys_hbm, gates_ref, h_ref, shared_ref, gate_ref, g_ref, b_ref = refs[:8]
    rest = list(refs[8:])
    if has_next:
        sc_ref, sh_ref = rest[:2]
        rest = rest[2:]
    out_refs, (idx_ref, gbuf_ref, sem_idx, sems) = rest[:2], rest[2:]
    t = pl.program_id(0)
    tm = h_ref.shape[0]
    idx_copy = pltpu.make_async_copy(dest_hbm.at[t], idx_ref, sem_idx)
    idx_copy.start()
    idx_copy.wait()

    def row_copy(k, c, src):
        return pltpu.make_async_copy(ys_hbm.at[pl.ds(src, 1), :], gbuf_ref.at[k, pl.ds(c, 1), :], sems.at[k])

    for k in range(TOP_K):
        def issue(c, carry, k=k):
            row_copy(k, c, idx_ref[k, c]).start()
            return carry

        lax.fori_loop(0, tm, issue, 0, unroll=8)

    acc_lo = jnp.zeros((tm, gbuf_ref.shape[2]), F32)
    acc_hi = jnp.zeros((tm, gbuf_ref.shape[2]), F32)
    for k in range(TOP_K):
        def drain(i, carry, k=k):
            row_copy(k, 0, 0).wait()
            return carry

        lax.fori_loop(0, tm, drain, 0, unroll=16)
        lo, hi = _unpack_bf16_pairs(gbuf_ref[k])
        gk = gates_ref[:, k:k + 1]
        acc_lo = acc_lo + gk * lo
        acc_hi = acc_hi + gk * hi
    ffn = jnp.concatenate([acc_lo, acc_hi], axis=1) + shared_ref[...].astype(F32)
    y = _layer_norm(alpha * h_ref[...] + gate_ref[...] * ffn, g_ref[...], b_ref[...])
    if has_next:
        hn_ref, u_ref = out_refs
        hn_ref[...] = y
        u_ref[...] = (y * (1.0 + sc_ref[...]) + sh_ref[...]).astype(BF16)
    else:
        first_ref, second_ref = out_refs

        @pl.when(t < split_at)
        def _():
            first_ref[...] = y

        @pl.when(t >= split_at)
        def _():
            second_ref[...] = y


def _combine(stream, dest, y_sorted, gates_t, h, shared, mod_l, ln_g, ln_b, alpha, next_mod):
    n, d = h.shape
    d2 = y_sorted.shape[1]
    tm = COMBINE_TILE
    n_steps = n // tm
    assert stream.n_ctx % tm == 0
    split_at = stream.n_ctx // tm
    has_next = next_mod is not None
    dest_tiles = dest.reshape(TOP_K, n_steps, tm).transpose(1, 0, 2)
    tile = pl.BlockSpec((tm, d), lambda t: (t, 0))
    in_specs = [pl.BlockSpec(memory_space=pl.ANY), pl.BlockSpec(memory_space=pl.ANY),
                pl.BlockSpec((tm, TOP_K), lambda t: (t, 0)), tile, tile,
                _mod_spec(stream, d, 5, tm), _row_spec(d), _row_spec(d)]
    args = [dest_tiles, y_sorted, gates_t.T, h, shared, mod_l, ln_g.reshape(1, d), ln_b.reshape(1, d)]
    if has_next:
        in_specs += [_mod_spec(stream, d, 1, tm), _mod_spec(stream, d, 0, tm)]
        args += [next_mod, next_mod]
        out_specs = [tile, tile]
        out_shape = [jax.ShapeDtypeStruct((n, d), F32), jax.ShapeDtypeStruct((n, d), BF16)]
    else:
        out_specs = [pl.BlockSpec((tm, d), lambda t: (jnp.minimum(t, split_at - 1), 0)),
                     pl.BlockSpec((tm, d), lambda t: (jnp.maximum(t - split_at, 0), 0))]
        out_shape = [jax.ShapeDtypeStruct((stream.n_ctx, d), F32),
                     jax.ShapeDtypeStruct((n - stream.n_ctx, d), F32)]
    return pl.pallas_call(
        functools.partial(_combine_kernel, alpha=alpha, has_next=has_next, split_at=split_at),
        grid=(n_steps,),
        in_specs=in_specs, out_specs=out_specs, out_shape=out_shape,
        scratch_shapes=[pltpu.SMEM((TOP_K, tm), jnp.int32), pltpu.VMEM((TOP_K, tm, d2), jnp.uint32),
                        pltpu.SemaphoreType.DMA, pltpu.SemaphoreType.DMA((TOP_K,))],
        compiler_params=_params(1),
        name="moe_combine_ln",
    )(*args)


def _moe_sublayer(stream, h, u_packed, mod_l, layer, w_router, b_router, w_exp_gate, w_exp_up, w_exp_down,
                  w_sg, w_su, w_sd, ln_g, ln_b, alpha, next_mod):
    n = h.shape[0]
    e_t, rank_t, gates_t, counts = _route(stream, h, mod_l, w_router, b_router)
    dest, fill_start, fill_count, block_expert, block_active, n_blocks = _moe_tables(
        e_t, rank_t, counts[:, 0], n * TOP_K)
    x_sorted = _dispatch(stream, u_packed, dest, fill_start, fill_count, n_blocks * MOE_BLOCK)
    y_sorted = _expert_ffn(x_sorted, block_expert, block_active, n_blocks,
                           w_exp_gate, w_exp_up, w_exp_down, layer)
    shared = _shared_ffn(u_packed, w_sg, w_su, w_sd)
    return _combine(stream, dest, y_sorted, gates_t, h, shared, mod_l, ln_g, ln_b, alpha, next_mod)


def kernel(x_prompt, x_sample, c, state_gla, c_ctx, ln_in_g, ln_in_b, w_ada, b_ada, w_gla_in, w_gla_a1,
           w_gla_a2, b_gla_a, gla_norm_g, w_gla_out, w_fnet_out, ln_g, ln_b, w_router, b_router,
           w_exp_gate, w_exp_up, w_exp_down, w_sh_gate, w_sh_up, w_sh_down):
    n_ctx_seq, ctx_len, d = x_prompt.shape
    n_lat_seq, lat_len, _ = x_sample.shape
    depth = w_ada.shape[0]
    _, _, _, heads, dk, dv = state_gla.shape
    stream = _Stream(n_ctx_seq, ctx_len, n_lat_seq, lat_len)
    assert 1 + n_lat_seq <= 8
    alpha = (2 * depth) ** 0.25

    cond8 = jnp.zeros((8, d), F32).at[0].set(c_ctx).at[1:1 + n_lat_seq].set(c)
    mod = _ada_mod(cond8, w_ada, b_ada).reshape(depth, 8, N_MOD, 1, d)

    h, u = _embed(stream, x_prompt.reshape(-1, d), x_sample.reshape(-1, d), ln_in_g, ln_in_b, mod[0])

    n_gla = state_gla.shape[1]
    states = None
    for i in range(depth):
        j = i // 2
        if i % 2 == 0:
            mix, states = _gla_mixer(stream, u, state_gla, states, j, n_gla, w_gla_in, w_gla_a1[j],
                                     w_gla_a2[j], b_gla_a[j], gla_norm_g[j], w_gla_out, heads, dk, dv)
        else:
            mix = _fnet_mixer(stream, u, w_fnet_out, j)
        h, u_packed = _lnres(stream, h, [mix], mod[i], 2, ln_g[i, 0], ln_b[i, 0], alpha, mod[i], (3, 4),
                             packed=True)
        last = i == depth - 1
        h, u = _moe_sublayer(stream, h, u_packed, mod[i], i, w_router[i], b_router[i],
                             w_exp_gate, w_exp_up, w_exp_down, w_sh_gate[i], w_sh_up[i], w_sh_down[i],
                             ln_g[i, 1], ln_b[i, 1], alpha, None if last else mod[i + 1])

    return h.reshape(x_prompt.shape), u.reshape(x_sample.shape), states
```
